```python
import math
import jax
import jax.numpy as jnp
from jax import lax
import numpy as np

D_MODEL = 1024
BATCH = 8
SEQ = 4096
DEPTH = 2

ATT_HEADS = 4
ATT_HEAD_DIM = 64
ATT_WIDTH = ATT_HEADS * ATT_HEAD_DIM
DILATED_PATTERNS = ((128, 1), (512, 4), (2048, 16))
POOL_WINDOWS = (2, 4, 8, 16)
POOL_GROUP_DIM = 64
POOL_WIDTH = len(POOL_WINDOWS) * POOL_GROUP_DIM
DN_HEADS = 4
DN_HEAD_DIM = 128
DN_WIDTH = DN_HEADS * DN_HEAD_DIM
DN_CONV = 4
DN_CHUNK = 64
MIX_WIDTH = ATT_WIDTH + POOL_WIDTH + DN_WIDTH
IN_SPLITS = (ATT_WIDTH, ATT_WIDTH, ATT_WIDTH, POOL_WIDTH,
             DN_WIDTH, DN_WIDTH, DN_WIDTH, DN_WIDTH, DN_HEADS, DN_HEADS)
IN_WIDTH = sum(IN_SPLITS)
D_FF = 2816
ROPE_THETA = 10000.0
EPS = 1e-6

kernel_name = "hybrid_dilated_pool_deltanet_macaron"


def rms_norm(x, w):
    xf = x.astype(jnp.float32)
    y = xf * lax.rsqrt(jnp.mean(xf * xf, axis=-1, keepdims=True) + EPS)
    return (y * w).astype(x.dtype)


def swiglu(h, w_gate, w_up, w_down):
    return (jax.nn.silu(h @ w_gate) * (h @ w_up)) @ w_down


def rotary_tables(positions, dim):
    inv_freq = ROPE_THETA ** (-jnp.arange(0, dim, 2, dtype=jnp.float32) / dim)
    ang = positions.astype(jnp.float32)[..., None] * inv_freq
    return jnp.cos(ang)[:, :, None, :], jnp.sin(ang)[:, :, None, :]


def apply_rope(t, cos, sin):
    t1, t2 = jnp.split(t.astype(jnp.float32), 2, axis=-1)
    return jnp.concatenate([t1 * cos - t2 * sin, t2 * cos + t1 * sin], axis=-1).astype(t.dtype)


def dilated_window_attention(q, k, v, window, dilation):
    B, S, H, E = q.shape
    n = window // dilation
    span = n * dilation
    L = -(-S // span) * span
    M = L // dilation
    nb = M // n

    def to_blocks(t):
        t = jnp.pad(t, ((0, 0), (0, L - S), (0, 0), (0, 0)))
        return t.reshape(B, nb, n, dilation, H, E)

    def with_prev(t):
        prev = jnp.pad(t, ((0, 0), (1, 0), (0, 0), (0, 0), (0, 0), (0, 0)))[:, :-1]
        return jnp.concatenate([prev, t], axis=2)

    qb = to_blocks(q)
    kk = with_prev(to_blocks(k))
    vv = with_prev(to_blocks(v))
    s = jnp.einsum('bcqrhe,bckrhe->bcrhqk', qb, kk).astype(jnp.float32) / math.sqrt(E)
    qi = jnp.arange(n)[:, None]
    ki = jnp.arange(2 * n)[None, :]
    dist = n + qi - ki
    blk = jnp.arange(nb)[:, None, None]
    valid = (dist >= 0) & (dist <= n) & ((blk - 1) * n + ki >= 0)
    s = jnp.where(valid[None, :, None, None], s, -jnp.inf)
    m = jnp.max(s, axis=-1, keepdims=True)
    p = jnp.exp(s - m)
    den = jnp.sum(p, axis=-1, keepdims=True)
    o = jnp.einsum('bcrhqk,bckrhe->bcqrhe', (p / den).astype(vv.dtype), vv)
    lse = (m + jnp.log(den))[..., 0]
    o = o.reshape(B, L, H, E)[:, :S]
    lse = lse.transpose(0, 1, 4, 2, 3).reshape(B, L, H)[:, :S]
    return o, lse


def dilated_attention(q, k, v):
    outs, lses = [], []
    for window, dilation in DILATED_PATTERNS:
        o, lse = dilated_window_attention(q, k, v, window, dilation)
        outs.append(o)
        lses.append(lse)
    wts = jax.nn.softmax(jnp.stack(lses, axis=0), axis=0)
    y = sum(wts[i][..., None] * outs[i].astype(jnp.float32) for i in range(len(outs)))
    return y.astype(q.dtype)


def multiscale_pool(u, pool_w, pool_scale):
    B, S, _ = u.shape
    G = len(POOL_WINDOWS)
    ug = u.reshape(B, S, G, POOL_GROUP_DIM)
    cs = jnp.cumsum(ug.astype(jnp.float32), axis=1)
    t = jnp.arange(S)
    pooled = []
    for g, w in enumerate(POOL_WINDOWS):
        csg = cs[:, :, g]
        lower = jnp.pad(csg, ((0, 0), (w, 0), (0, 0)))[:, :S]
        count = jnp.minimum(t + 1, w).astype(jnp.float32)[None, :, None]
        pooled.append((csg - lower) / count)
    pooled = jnp.stack(pooled, axis=2) - ug.astype(jnp.float32)
    y = jnp.einsum('bsgc,gcd->bsgd', pooled.astype(u.dtype), pool_w)
    return y.reshape(B, S, POOL_WIDTH) * pool_scale


def causal_depthwise_conv(u, w):
    K = w.shape[0]
    up = jnp.pad(u, ((0, 0), (K - 1, 0), (0, 0)))
    return lax.conv_general_dilated(up, w[:, None, :], window_strides=(1,), padding='VALID',
                                    dimension_numbers=('NWC', 'WIO', 'NWC'),
                                    feature_group_count=u.shape[-1])


def l2_normalize(t):
    tf = t.astype(jnp.float32)
    return tf * lax.rsqrt(jnp.sum(tf * tf, axis=-1, keepdims=True) + EPS)


def chunk_gated_delta_rule(q, k, v, g, beta):
    B, S, H, Dk = q.shape
    Dv = v.shape[-1]
    C = DN_CHUNK
    N = S // C
    to_c = lambda t: t.astype(jnp.float32).reshape(B, N, C, H, -1).transpose(1, 0, 3, 2, 4)
    qc, kc, vc = to_c(q), to_c(k), to_c(v)
    beta = beta.astype(jnp.float32).reshape(B, N, C, H).transpose(1, 0, 3, 2)
    g = jnp.cumsum(g.astype(jnp.float32).reshape(B, N, C, H).transpose(1, 0, 3, 2), axis=-1)
    kb = kc * beta[..., None]
    vb = vc * beta[..., None]
    lower = jnp.tril(jnp.ones((C, C), dtype=bool))
    strict = jnp.tril(jnp.ones((C, C), dtype=bool), -1)
    diff = g[..., :, None] - g[..., None, :]
    decay = jnp.where(lower, jnp.exp(jnp.where(lower, diff, 0.0)), 0.0)
    A = jnp.where(strict, jnp.einsum('nbhid,nbhjd->nbhij', kb, kc) * decay, 0.0)
    eye = jnp.eye(C, dtype=jnp.float32)
    T = lax.linalg.triangular_solve(eye + A, jnp.broadcast_to(eye, A.shape),
                                    left_side=True, lower=True)
    u = T @ vb
    w = T @ (kb * jnp.exp(g)[..., None])
    intra = jnp.where(lower, jnp.einsum('nbhid,nbhjd->nbhij', qc, kc) * decay, 0.0)

    def step(state, xs):
        q_i, k_i, u_i, w_i, g_i, a_i = xs
        v_new = u_i - w_i @ state
        o = (q_i * jnp.exp(g_i)[..., None]) @ state + a_i @ v_new
        g_last = g_i[..., -1]
        k_dec = k_i * jnp.exp(g_last[..., None] - g_i)[..., None]
        state = state * jnp.exp(g_last)[..., None, None] + jnp.einsum('bhck,bhcv->bhkv', k_dec, v_new)
        return state, o

    state0 = jnp.zeros((B, H, Dk, Dv), dtype=jnp.float32)
    _, o = lax.scan(step, state0, (qc, kc, u, w, g, intra))
    return o.transpose(1, 0, 3, 2, 4).reshape(B, S, H, Dv)


def gated_deltanet(q, k, v, z, b, a, conv_w, a_log, dt_bias, norm_w):
    B, S, _ = q.shape
    qkv = jax.nn.silu(causal_depthwise_conv(jnp.concatenate([q, k, v], axis=-1), conv_w))
    q, k, v = jnp.split(qkv, 3, axis=-1)
    q = l2_normalize(q.reshape(B, S, DN_HEADS, DN_HEAD_DIM)) * (DN_HEAD_DIM ** -0.5)
    k = l2_normalize(k.reshape(B, S, DN_HEADS, DN_HEAD_DIM))
    v = v.reshape(B, S, DN_HEADS, DN_HEAD_DIM)
    beta = jax.nn.sigmoid(b.astype(jnp.float32))
    g = -jnp.exp(a_log.astype(jnp.float32)) * jax.nn.softplus(a.astype(jnp.float32) + dt_bias)
    o = chunk_gated_delta_rule(q, k, v, g, beta)
    o = rms_norm(o, norm_w) * jax.nn.silu(z.reshape(B, S, DN_HEADS, DN_HEAD_DIM).astype(jnp.float32))
    return o.reshape(B, S, DN_WIDTH).astype(z.dtype)


def setup_inputs(seed: int = 0) -> dict:
    key = jax.random.key(seed)
    ks = jax.random.split(key, 24)
    nrm = lambda kk, shape, fan_in: jax.random.normal(kk, shape, jnp.float32) * (fan_in ** -0.5)
    gain = lambda kk, shape: 1.0 + 0.05 * jax.random.normal(kk, shape, jnp.float32)
    L = DEPTH
    x = jax.random.normal(ks[0], (BATCH, SEQ, D_MODEL), jnp.float32)
    positions = jnp.broadcast_to(jnp.arange(SEQ, dtype=jnp.int32), (BATCH, SEQ))
    dt = jnp.exp(jax.random.uniform(ks[12], (L, DN_HEADS), jnp.float32,
                                    math.log(1e-3), math.log(1e-1)))
    dt_bias = dt + jnp.log(-jnp.expm1(-dt))
    a_log = jnp.log(jax.random.uniform(ks[11], (L, DN_HEADS), jnp.float32, 1.0, 16.0))
    return {
        "x": x,
        "positions": positions,
        "ffn1_norm": gain(ks[1], (L, D_MODEL)),
        "ffn1_w_gate": nrm(ks[2], (L, D_MODEL, D_FF), D_MODEL),
        "ffn1_w_up": nrm(ks[3], (L, D_MODEL, D_FF), D_MODEL),
        "ffn1_w_down": nrm(ks[4], (L, D_FF, D_MODEL), D_FF),
        "mix_norm": gain(ks[5], (L, D_MODEL)),
        "w_in": nrm(ks[6], (L, D_MODEL, IN_WIDTH), D_MODEL),
        "pool_w": nrm(ks[7], (L, len(POOL_WINDOWS), POOL_GROUP_DIM, POOL_GROUP_DIM), POOL_GROUP_DIM),
        "pool_scale": gain(ks[8], (L, POOL_WIDTH)),
        "dn_conv_w": nrm(ks[9], (L, DN_CONV, 3 * DN_WIDTH), DN_CONV),
        "dn_a_log": a_log,
        "dn_dt_bias": dt_bias,
        "dn_out_norm": gain(ks[13], (L, DN_HEAD_DIM)),
        "w_out": nrm(ks[14], (L, MIX_WIDTH, D_MODEL), MIX_WIDTH),
        "ffn2_norm": gain(ks[15], (L, D_MODEL)),
        "ffn2_w_gate": nrm(ks[16], (L, D_MODEL, D_FF), D_MODEL),
        "ffn2_w_up": nrm(ks[17], (L, D_MODEL, D_FF), D_MODEL),
        "ffn2_w_down": nrm(ks[18], (L, D_FF, D_MODEL), D_FF),
        "final_norm": gain(ks[19], (D_MODEL,)),
    }


def reference(x, positions, ffn1_norm, ffn1_w_gate, ffn1_w_up, ffn1_w_down, mix_norm, w_in,
              pool_w, pool_scale, dn_conv_w, dn_a_log, dn_dt_bias, dn_out_norm, w_out,
              ffn2_norm, ffn2_w_gate, ffn2_w_up, ffn2_w_down, final_norm):
    B, S, _ = x.shape
    cos, sin = rotary_tables(positions, ATT_HEAD_DIM)
    split_at = np.cumsum(IN_SPLITS)[:-1].tolist()
    for l in range(DEPTH):
        h = rms_norm(x, ffn1_norm[l])
        x = x + 0.5 * swiglu(h, ffn1_w_gate[l], ffn1_w_up[l], ffn1_w_down[l])
        h = rms_norm(x, mix_norm[l])
        proj = h @ w_in[l]
        aq, ak, av, pu, dq, dk, dv, dz, db, da = jnp.split(proj, split_at, axis=-1)
        aq = apply_rope(aq.reshape(B, S, ATT_HEADS, ATT_HEAD_DIM), cos, sin)
        ak = apply_rope(ak.reshape(B, S, ATT_HEADS, ATT_HEAD_DIM), cos, sin)
        av = av.reshape(B, S, ATT_HEADS, ATT_HEAD_DIM)
        ya = dilated_attention(aq, ak, av).reshape(B, S, ATT_WIDTH)
        yb = multiscale_pool(pu, pool_w[l], pool_scale[l])
        yc = gated_deltanet(dq, dk, dv, dz, db, da, dn_conv_w[l], dn_a_log[l],
                            dn_dt_bias[l], dn_out_norm[l])
        x = x + jnp.concatenate([ya, yb, yc], axis=-1) @ w_out[l]
        h = rms_norm(x, ffn2_norm[l])
        x = x + 0.5 * swiglu(h, ffn2_w_gate[l], ffn2_w_up[l], ffn2_w_down[l])
    return rms_norm(x, final_norm)
```

```python
import functools
import math

import jax
import jax.numpy as jnp
from jax import lax
from jax.experimental import pallas as pl
from jax.experimental.pallas import tpu as pltpu

ATT_HEADS = 4
ATT_HEAD_DIM = 64
ATT_WIDTH = ATT_HEADS * ATT_HEAD_DIM
ATT_BLOCK = 128
DILATIONS = (1, 4, 16)
POOL_WINDOWS = (2, 4, 8, 16)
POOL_GROUP_DIM = 64
POOL_WIDTH = len(POOL_WINDOWS) * POOL_GROUP_DIM
POOL_HALO = 16
DN_HEADS = 4
DN_HEAD_DIM = 128
DN_WIDTH = DN_HEADS * DN_HEAD_DIM
DN_CONV = 4
DN_CHUNK = 64
GATE_PAD = 128
ROPE_THETA = 10000.0
EPS = 1e-6

LANES = 128
FF_CHUNK = 256
TOKEN_TILE = 512
VMEM_LIMIT = 56 * 1024 * 1024

BF16 = jnp.bfloat16
F32 = jnp.float32


def _dot(a, b):
    return jnp.dot(a, b, preferred_element_type=F32)


def _dot_nt(a, b):
    return lax.dot_general(a, b, (((1,), (1,)), ((), ())), preferred_element_type=F32)


def _dot_tn(a, b):
    return lax.dot_general(a, b, (((0,), (0,)), ((), ())), preferred_element_type=F32)


def _sigmoid(x):
    return 1.0 / (1.0 + jnp.exp(-x))


def _rms(x, w):
    return x * lax.rsqrt(jnp.mean(x * x, axis=-1, keepdims=True) + EPS) * w


def _resident(shape):
    return pl.BlockSpec(shape, lambda *_: (0,) * len(shape), pipeline_mode=pl.Buffered(1))


def _params(*sem):
    return pltpu.CompilerParams(dimension_semantics=sem, vmem_limit_bytes=VMEM_LIMIT)


def _rope_table_kernel(pos_ref, freq_ref, cos_ref, sin_ref):
    ang = pos_ref[...].astype(F32) * freq_ref[...]
    cos_ref[...] = jnp.cos(ang)
    sin_ref[...] = jnp.sin(ang)


def _rope_tables(positions):
    m = positions.size
    half = ATT_HEAD_DIM // 2
    per_row = LANES // half
    inv_freq = ROPE_THETA ** (-jnp.arange(0, ATT_HEAD_DIM, 2, dtype=F32) / ATT_HEAD_DIM)
    freq = jnp.tile(inv_freq, per_row)[None, :]
    pos = jnp.repeat(positions.reshape(m // per_row, per_row), half, axis=1)
    rows = m // per_row
    tile = min(rows, 1024)
    cos, sin = pl.pallas_call(
        _rope_table_kernel,
        grid=(rows // tile,),
        in_specs=[pl.BlockSpec((tile, LANES), lambda i: (i, 0)), _resident((1, LANES))],
        out_specs=[pl.BlockSpec((tile, LANES), lambda i: (i, 0))] * 2,
        out_shape=[jax.ShapeDtypeStruct((rows, LANES), F32)] * 2,
        compiler_params=_params("parallel"),
        name="rope_tables",
    )(pos, freq)
    reps = ATT_WIDTH // half
    return jnp.tile(cos.reshape(m, half), (1, reps)), jnp.tile(sin.reshape(m, half), (1, reps))


def _ffn_kernel(x_ref, nw_ref, wg_ref, wu_ref, wd_ref, fn_ref, o_ref, act_ref, *, final):
    x = x_ref[...]
    h = _rms(x, nw_ref[...]).astype(BF16)
    for c in range(act_ref.shape[1] // FF_CHUNK):
        cols = slice(c * FF_CHUNK, (c + 1) * FF_CHUNK)
        g = _dot(h, wg_ref[:, cols])
        u = _dot(h, wu_ref[:, cols])
        act_ref[:, cols] = (g * _sigmoid(g) * u).astype(BF16)
    y = x + 0.5 * _dot(act_ref[...], wd_ref[...])
    if final:
        y = _rms(y, fn_ref[...])
    o_ref[...] = y


def _ffn(x, norm_w, w_gate, w_up, w_down, final_w, *, final):
    m, d = x.shape
    d_ff = w_gate.shape[1]
    tm = min(TOKEN_TILE, m)
    row = lambda i: (i, 0)
    return pl.pallas_call(
        functools.partial(_ffn_kernel, final=final),
        grid=(m // tm,),
        in_specs=[pl.BlockSpec((tm, d), row), _resident((1, d)), _resident((d, d_ff)),
                  _resident((d, d_ff)), _resident((d_ff, d)), _resident((1, d))],
        out_specs=pl.BlockSpec((tm, d), row),
        out_shape=jax.ShapeDtypeStruct((m, d), F32),
        scratch_shapes=[pltpu.VMEM((tm, d_ff), BF16)],
        compiler_params=_params("parallel"),
        name="ffn",
    )(x, norm_w[None, :], w_gate.astype(BF16), w_up.astype(BF16), w_down.astype(BF16), final_w[None, :])


def _rope(t, cos, sin):
    half = ATT_HEAD_DIM // 2
    width = t.shape[1]
    lane = lax.broadcasted_iota(jnp.int32, t.shape, 1)
    first = (lane % ATT_HEAD_DIM) < half
    partner = jnp.where(first, -pltpu.roll(t, width - half, axis=1), pltpu.roll(t, half, axis=1))
    return t * cos + partner * sin


def _inproj_kernel(x_ref, nw_ref, w_ref, cos_ref, sin_ref, att_ref, pu_ref, dqkv_ref, dz_ref, dba_ref):
    h = _rms(x_ref[...], nw_ref[...]).astype(BF16)
    cos = cos_ref[...]
    sin = sin_ref[...]
    a = ATT_WIDTH
    proj = lambda lo, hi: _dot(h, w_ref[:, lo:hi])
    att_ref[:, 0:a] = _rope(proj(0, a), cos, sin) * (ATT_HEAD_DIM ** -0.5)
    att_ref[:, a:2 * a] = _rope(proj(a, 2 * a), cos, sin)
    att_ref[:, 2 * a:3 * a] = proj(2 * a, 3 * a)
    base = 3 * a
    pu_ref[...] = proj(base, base + POOL_WIDTH)
    base += POOL_WIDTH
    for c in range(3):
        dqkv_ref[:, c * DN_WIDTH:(c + 1) * DN_WIDTH] = proj(base + c * DN_WIDTH, base + (c + 1) * DN_WIDTH)
    base += 3 * DN_WIDTH
    dz_ref[...] = proj(base, base + DN_WIDTH)
    base += DN_WIDTH
    dba_ref[...] = proj(base, base + GATE_PAD)


def _inproj(x, norm_w, w_in, cos, sin):
    m, d = x.shape
    tm = min(TOKEN_TILE, m)
    w = jnp.pad(w_in, ((0, 0), (0, GATE_PAD - 2 * DN_HEADS))).astype(BF16)
    widths = (3 * ATT_WIDTH, POOL_WIDTH, 3 * DN_WIDTH, DN_WIDTH, GATE_PAD)
    assert w.shape[1] == sum(widths)
    row = lambda i: (i, 0)
    return pl.pallas_call(
        _inproj_kernel,
        grid=(m // tm,),
        in_specs=[pl.BlockSpec((tm, d), row), _resident((1, d)), _resident(w.shape),
                  pl.BlockSpec((tm, ATT_WIDTH), row), pl.BlockSpec((tm, ATT_WIDTH), row)],
        out_specs=[pl.BlockSpec((tm, n), row) for n in widths],
        out_shape=[jax.ShapeDtypeStruct((m, n), F32) for n in widths],
        compiler_params=_params("parallel"),
        name="inproj",
    )(x, norm_w[None, :], w, cos, sin)


def _attn_kernel(q_ref, k_ref, v_ref, o_ref, m_acc, l_acc, o_acc):
    seq = q_ref.shape[0]
    n = ATT_BLOCK
    lane = lax.broadcasted_iota(jnp.int32, (n, LANES), 1)
    head0 = lane < ATT_HEAD_DIM
    qi = lax.broadcasted_iota(jnp.int32, (2 * n, 2 * n), 0) % n
    ki = lax.broadcasted_iota(jnp.int32, (2 * n, 2 * n), 1)

    for d in DILATIONS:
        span = n * d

        def rows(start):
            if d == 1:
                return pl.ds(pl.multiple_of(start, n), n)
            return pl.ds(start, n, stride=d)

        def body(idx, carry):
            c = idx // d
            r = idx % d
            cur = c * span + r
            prev = jnp.maximum(cur - span, 0)
            q = q_ref[rows(cur), :]
            kc = jnp.concatenate([k_ref[rows(prev), :], k_ref[rows(cur), :]], axis=0).astype(BF16)
            vc = jnp.concatenate([v_ref[rows(prev), :], v_ref[rows(cur), :]], axis=0).astype(BF16)
            q2 = jnp.concatenate([jnp.where(head0, q, 0.0), jnp.where(head0, 0.0, q)], axis=0).astype(BF16)
            s = _dot_nt(q2, kc)
            first_key = jnp.maximum(qi, jnp.where(c > 0, 0, n))
            s = jnp.where((ki >= first_key) & (ki <= qi + n), s, -jnp.inf)
            mx = jnp.max(s, axis=1, keepdims=True)
            p = jnp.exp(s - mx)
            den = jnp.sum(p, axis=1, keepdims=True)
            o2 = _dot(p.astype(BF16), vc)
            m_new = jnp.where(head0, mx[:n], mx[n:])
            l_new = jnp.where(head0, den[:n], den[n:])
            o_new = jnp.where(head0, o2[:n], o2[n:])
            if d == DILATIONS[0]:
                m_acc[rows(cur), :] = m_new
                l_acc[rows(cur), :] = l_new
                o_acc[rows(cur), :] = o_new
            else:
                m_old = m_acc[rows(cur), :]
                m_tot = jnp.maximum(m_old, m_new)
                a_old = jnp.exp(m_old - m_tot)
                a_new = jnp.exp(m_new - m_tot)
                m_acc[rows(cur), :] = m_tot
                l_acc[rows(cur), :] = l_acc[rows(cur), :] * a_old + l_new * a_new
                o_acc[rows(cur), :] = o_acc[rows(cur), :] * a_old + o_new * a_new
            return carry

        lax.fori_loop(0, seq // n, body, 0)

    o_ref[...] = (o_acc[...] / l_acc[...]).astype(o_ref.dtype)


def _attention(att, batch, seq):
    att = att.reshape(batch, seq, 3 * ATT_WIDTH)
    slabs = ATT_WIDTH // LANES
    assert seq % (ATT_BLOCK * max(DILATIONS)) == 0
    spec = lambda off: pl.BlockSpec((None, seq, LANES), lambda b, s: (b, 0, off + s))
    out = pl.pallas_call(
        _attn_kernel,
        grid=(batch, slabs),
        in_specs=[spec(0), spec(slabs), spec(2 * slabs)],
        out_specs=spec(0),
        out_shape=jax.ShapeDtypeStruct((batch, seq, ATT_WIDTH), BF16),
        scratch_shapes=[pltpu.VMEM((seq, LANES), F32)] * 3,
        compiler_params=_params("parallel", "parallel"),
        name="dilated_attention",
    )(att, att, att)
    return out.reshape(batch * seq, ATT_WIDTH)


def _dn_kernel(q_ref, k_ref, v_ref, z_ref, ba_ref, wq_ref, wk_ref, wv_ref, alog_ref, dt_ref, nw_ref,
               o_ref, state_ref):
    seq = q_ref.shape[1]
    c = DN_CHUNK
    head = pl.program_id(1)
    state_ref[...] = jnp.zeros_like(state_ref)

    ri = lax.broadcasted_iota(jnp.int32, (c, c), 0)
    ci = lax.broadcasted_iota(jnp.int32, (c, c), 1)
    lower = ri >= ci
    strict = ri > ci
    eye = ri == ci
    tri_ones = jnp.where(lower, 1.0, 0.0).astype(F32)
    ident = jnp.where(eye, 1.0, 0.0).astype(F32)
    gate_lane = lax.broadcasted_iota(jnp.int32, (c, GATE_PAD), 1)

    def conv_silu(ref, w_ref, i, r0):
        cur = ref[0, pl.ds(r0, c), :]
        prev = ref[0, pl.ds(jnp.maximum(r0 - 8, 0), 8), :] * jnp.where(i > 0, 1.0, 0.0)
        ext = jnp.concatenate([prev, cur], axis=0)
        w = w_ref[...]
        y = sum(w[j:j + 1, :] * ext[8 - (DN_CONV - 1) + j:8 - (DN_CONV - 1) + j + c, :] for j in range(DN_CONV))
        return y * _sigmoid(y)

    def l2n(t):
        return t * lax.rsqrt(jnp.sum(t * t, axis=-1, keepdims=True) + EPS)

    def body(i, carry):
        r0 = pl.multiple_of(i * c, c)
        q = l2n(conv_silu(q_ref, wq_ref, i, r0)) * (DN_HEAD_DIM ** -0.5)
        k = l2n(conv_silu(k_ref, wk_ref, i, r0))
        v = conv_silu(v_ref, wv_ref, i, r0)

        ba = ba_ref[0, pl.ds(r0, c), :]
        beta = jnp.sum(jnp.where(gate_lane == head, _sigmoid(ba), 0.0), axis=1, keepdims=True)
        x = ba + dt_ref[...]
        softplus = jnp.maximum(x, 0.0) + jnp.log(1.0 + jnp.exp(-jnp.abs(x)))
        g_all = -jnp.exp(alog_ref[...]) * softplus
        g = jnp.sum(jnp.where(gate_lane == head + DN_HEADS, g_all, 0.0), axis=1, keepdims=True)
        gc = jnp.dot(tri_ones, jnp.broadcast_to(g, (c, DN_HEAD_DIM)), precision=lax.Precision.HIGHEST,
                     preferred_element_type=F32)
        gc_row = jnp.sum(jnp.where(eye, gc[:, :c], 0.0), axis=0, keepdims=True)
        diff = gc[:, :c] - gc_row
        decay = jnp.where(lower, jnp.exp(jnp.where(lower, diff, 0.0)), 0.0)

        kb = k * beta
        vb = v * beta
        k16 = k.astype(BF16)
        kq = _dot_nt(jnp.concatenate([kb, q], axis=0).astype(BF16), k16)
        a_mat = jnp.where(strict, kq[:c] * decay, 0.0)
        intra = jnp.where(lower, kq[c:] * decay, 0.0)

        pw = -a_mat
        t_inv = ident + pw
        for _ in range(int(math.log2(c)) - 1):
            pw16 = pw.astype(BF16)
            pw = _dot(pw16, pw16)
            t_inv = t_inv + _dot(t_inv.astype(BF16), pw.astype(BF16))

        egc = jnp.exp(gc)
        uw = _dot(t_inv.astype(BF16), jnp.concatenate([vb, kb * egc], axis=1).astype(BF16))
        u = uw[:, :DN_HEAD_DIM]
        w = uw[:, DN_HEAD_DIM:]

        state = state_ref[...]
        s16 = state.astype(BF16)
        ws_qs = _dot(jnp.concatenate([w, q * egc], axis=0).astype(BF16), s16)
        v_new = u - ws_qs[:c]
        v16 = v_new.astype(BF16)
        o = ws_qs[c:] + _dot(intra.astype(BF16), v16)
        g_last = gc[c - 1:c, :]
        k_dec = k * jnp.exp(g_last - gc)
        state_ref[...] = state * jnp.exp(g_last) + _dot_tn(k_dec.astype(BF16), v16)

        z = z_ref[0, pl.ds(r0, c), :]
        o_ref[0, pl.ds(r0, c), :] = (_rms(o, nw_ref[...]) * (z * _sigmoid(z))).astype(o_ref.dtype)
        return carry

    lax.fori_loop(0, seq // c, body, 0)


def _deltanet(dqkv, dz, dba, conv_w, a_log, dt_bias, norm_w, batch, seq):
    h = DN_HEADS
    dqkv = dqkv.reshape(batch, seq, 3 * DN_WIDTH)
    dz = dz.reshape(batch, seq, DN_WIDTH)
    dba = dba.reshape(batch, seq, GATE_PAD)
    alog = jnp.pad(a_log, (h, GATE_PAD - 2 * h))[None, :]
    dt = jnp.pad(dt_bias, (h, GATE_PAD - 2 * h))[None, :]
    slab = lambda off: pl.BlockSpec((1, seq, DN_HEAD_DIM), lambda b, j: (b, 0, off + j))
    wspec = lambda off: pl.BlockSpec((DN_CONV, DN_HEAD_DIM), lambda b, j: (0, off + j))
    out = pl.pallas_call(
        _dn_kernel,
        grid=(batch, h),
        in_specs=[slab(0), slab(h), slab(2 * h), slab(0),
                  pl.BlockSpec((1, seq, GATE_PAD), lambda b, j: (b, 0, 0)),
                  wspec(0), wspec(h), wspec(2 * h),
                  _resident((1, GATE_PAD)), _resident((1, GATE_PAD)), _resident((1, DN_HEAD_DIM))],
        out_specs=slab(0),
        out_shape=jax.ShapeDtypeStruct((batch, seq, DN_WIDTH), BF16),
        scratch_shapes=[pltpu.VMEM((DN_HEAD_DIM, DN_HEAD_DIM), F32)],
        compiler_params=_params("parallel", "parallel"),
        name="gated_deltanet",
    )(dqkv, dqkv, dqkv, dz, dba, conv_w, conv_w, conv_w, alog, dt, norm_w[None, :])
    return out.reshape(batch * seq, DN_WIDTH)


def _outproj_kernel(x_ref, ya_ref, yc_ref, pu_ref, halo_ref, pw_ref, ps_ref, wo_ref, o_ref, *, tiles_per_seq):
    tm = x_ref.shape[0]
    t0 = (pl.program_id(0) % tiles_per_seq) * tm
    u = pu_ref[...]
    halo = halo_ref[...] * jnp.where(t0 > 0, 1.0, 0.0)
    sums = jnp.concatenate([halo, u], axis=0)
    lead = POOL_HALO
    pos = t0 + lax.broadcasted_iota(jnp.int32, (tm, POOL_WIDTH), 0)
    lane = lax.broadcasted_iota(jnp.int32, (tm, POOL_WIDTH), 1)
    pooled = jnp.zeros((tm, POOL_WIDTH), F32)
    width = 1
    for g, win in enumerate(POOL_WINDOWS):
        while width < win:
            sums = sums[width:] + sums[:-width]
            lead -= width
            width *= 2
        count = jnp.minimum(pos + 1, win).astype(F32)
        pooled = jnp.where(lane // POOL_GROUP_DIM == g, sums[lead:] / count, pooled)
    pooled = pooled - u
    yb = _dot(pooled.astype(BF16), pw_ref[...]) * ps_ref[...]
    y = jnp.concatenate([ya_ref[...], yb.astype(BF16), yc_ref[...]], axis=1)
    o_ref[...] = x_ref[...] + _dot(y, wo_ref[...])


def _outproj(x, ya, yc, pu, pool_w, pool_scale, w_out, seq):
    m, d = x.shape
    tm = min(TOKEN_TILE, seq)
    groups = len(POOL_WINDOWS)
    pw = jnp.zeros((POOL_WIDTH, POOL_WIDTH), F32)
    for g in range(groups):
        sl = slice(g * POOL_GROUP_DIM, (g + 1) * POOL_GROUP_DIM)
        pw = pw.at[sl, sl].set(pool_w[g])
    row = lambda i: (i, 0)
    halo_blocks = tm // POOL_HALO
    return pl.pallas_call(
        functools.partial(_outproj_kernel, tiles_per_seq=seq // tm),
        grid=(m // tm,),
        in_specs=[pl.BlockSpec((tm, d), row), pl.BlockSpec((tm, ATT_WIDTH), row),
                  pl.BlockSpec((tm, DN_WIDTH), row), pl.BlockSpec((tm, POOL_WIDTH), row),
                  pl.BlockSpec((POOL_HALO, POOL_WIDTH), lambda i: (jnp.maximum(i * halo_blocks - 1, 0), 0)),
                  _resident(pw.shape), _resident((1, POOL_WIDTH)), _resident(w_out.shape)],
        out_specs=pl.BlockSpec((tm, d), row),
        out_shape=jax.ShapeDtypeStruct((m, d), F32),
        compiler_params=_params("parallel"),
        name="pool_outproj",
    )(x, ya, yc, pu, pu, pw.astype(BF16), pool_scale[None, :], w_out.astype(BF16))


def kernel(x, positions, ffn1_norm, ffn1_w_gate, ffn1_w_up, ffn1_w_down, mix_norm, w_in, pool_w, pool_scale, dn_conv_w, dn_a_log, dn_dt_bias, dn_out_norm, w_out, ffn2_norm, ffn2_w_gate, ffn2_w_up, ffn2_w_down, final_norm):
    batch, seq, d = x.shape
    depth = w_in.shape[0]
    cos, sin = _rope_tables(positions)
    x = x.reshape(batch * seq, d)
    for l in range(depth):
        x = _ffn(x, ffn1_norm[l], ffn1_w_gate[l], ffn1_w_up[l], ffn1_w_down[l], final_norm, final=False)
        att, pu, dqkv, dz, dba = _inproj(x, mix_norm[l], w_in[l], cos, sin)
        ya = _attention(att, batch, seq)
        yc = _deltanet(dqkv, dz, dba, dn_conv_w[l], dn_a_log[l], dn_dt_bias[l], dn_out_norm[l], batch, seq)
        x = _outproj(x, ya, yc, pu, pool_w[l], pool_scale[l], w_out[l], seq)
        x = _ffn(x, ffn2_norm[l], ffn2_w_gate[l], ffn2_w_up[l], ffn2_w_down[l], final_norm,
                 final=(l == depth - 1))
    return x.reshape(batch, seq, d)
```
